```python
import math
import jax, jax.numpy as jnp
from jax import lax
import numpy as np

D_MODEL = 2048
BATCH = 4
SEQ = 4096
DEPTH = 4

GRID_W = 64
CTX_LEN = 256
HEAD_DIM = 128
EPS = 1e-6
POOL_WINDOWS = (2, 4, 8, 16)
POOL_WIDTH = D_MODEL // 2
POOL_GROUP = POOL_WIDTH // len(POOL_WINDOWS)
ATT_HEADS = D_MODEL // 256
ATT_KV_HEADS = 2
ATT_Q_W = ATT_HEADS * HEAD_DIM
ATT_KV_W = ATT_KV_HEADS * HEAD_DIM
ROPE_THETA = 10000.0
Q_BLOCK = 128
GDN_HEADS = D_MODEL // 256
GDN_DK = HEAD_DIM
GDN_DV = HEAD_DIM
GDN_QK_W = GDN_HEADS * GDN_DK
GDN_V_W = GDN_HEADS * GDN_DV
CONV_K = 4
CHUNK = 64
N_BRANCH = 3
BRANCH_W = D_MODEL // 2
N_EXPERTS = 16
EXPERT_FF = D_MODEL // 2
CAPACITY_FACTOR = 2
IN_SPLIT = (POOL_WIDTH, ATT_Q_W, ATT_KV_W, ATT_KV_W, GDN_QK_W, GDN_QK_W, GDN_V_W, GDN_V_W, 2 * GDN_HEADS, 2 * GDN_HEADS, N_BRANCH * D_MODEL)
N_IN = sum(IN_SPLIT)

kernel_name = 'hybrid_pool_gqa_gdn_ec_flow_block'

F32 = jnp.float32


def rmsnorm(x, g):
    xf = x.astype(F32)
    y = xf * lax.rsqrt(jnp.mean(xf * xf, axis=-1, keepdims=True) + EPS)
    return (y * g.astype(F32)).astype(x.dtype)


def l2norm(x):
    xf = x.astype(F32)
    return xf * lax.rsqrt(jnp.sum(xf * xf, axis=-1, keepdims=True) + EPS)


def modulate(h, shift, scale):
    return h * (1.0 + scale) + shift


def split_cols(p):
    idx = []
    acc = 0
    for s in IN_SPLIT[:-1]:
        acc += s
        idx.append(acc)
    return jnp.split(p, idx, axis=-1)


def axial_rope(row, col):
    n_freq = HEAD_DIM // 4
    inv = ROPE_THETA ** (-jnp.arange(n_freq, dtype=F32) / n_freq)
    ang = jnp.concatenate([row.astype(F32)[:, None] * inv, col.astype(F32)[:, None] * inv], axis=-1)
    return jnp.cos(ang), jnp.sin(ang)


def apply_rope(x, cos, sin):
    half = HEAD_DIM // 2
    xf = x.astype(F32)
    x1, x2 = xf[..., :half], xf[..., half:]
    c = cos[:, None, :]
    s = sin[:, None, :]
    return jnp.concatenate([x1 * c - x2 * s, x1 * s + x2 * c], axis=-1).astype(x.dtype)


def pool_mixer(u, pool_w, pool_scale):
    B, T, _ = u.shape
    uf = u.astype(F32)
    cs = jnp.concatenate([jnp.zeros((B, 1, POOL_WIDTH), F32), jnp.cumsum(uf, axis=1)], axis=1)
    t = jnp.arange(T)
    outs = []
    for gi, w in enumerate(POOL_WINDOWS):
        sl = slice(gi * POOL_GROUP, (gi + 1) * POOL_GROUP)
        lo = jnp.clip(t - w // 2, 0, T)
        hi = jnp.clip(t + w // 2, 0, T)
        win_sum = cs[:, hi, sl] - cs[:, lo, sl]
        cnt = (hi - lo).astype(F32)[:, None]
        outs.append(win_sum / cnt - uf[:, :, sl])
    y = jnp.stack(outs, axis=2).astype(u.dtype)
    y = jnp.einsum('btgc,gcd->btgd', y, pool_w).reshape(B, T, POOL_WIDTH)
    return y * pool_scale


def gqa_softmax(q, k, v):
    s = jnp.einsum('bqkgd,bskd->bkgqs', q, k, preferred_element_type=F32) * (HEAD_DIM ** -0.5)
    p = jax.nn.softmax(s, axis=-1).astype(v.dtype)
    return jnp.einsum('bkgqs,bskd->bqkgd', p, v)


def attention_mixer(q, k, v, q_c, k_c, v_c, q_norm_g, k_norm_g, cos, sin, need_ctx):
    B, T, _ = q.shape
    L = k_c.shape[1]
    grp = ATT_HEADS // ATT_KV_HEADS
    q = apply_rope(rmsnorm(q.reshape(B, T, ATT_HEADS, HEAD_DIM), q_norm_g), cos, sin)
    k = apply_rope(rmsnorm(k.reshape(B, T, ATT_KV_HEADS, HEAD_DIM), k_norm_g), cos, sin)
    k_c = rmsnorm(k_c.reshape(B, L, ATT_KV_HEADS, HEAD_DIM), k_norm_g)
    v_c = v_c.reshape(B, L, ATT_KV_HEADS, HEAD_DIM)
    keys = jnp.concatenate([k_c, k], axis=1)
    vals = jnp.concatenate([v_c, v.reshape(B, T, ATT_KV_HEADS, HEAD_DIM)], axis=1)
    nb = T // Q_BLOCK
    qb = jnp.moveaxis(q.reshape(B, nb, Q_BLOCK, ATT_KV_HEADS, grp, HEAD_DIM), 1, 0)
    o = lax.map(lambda blk: gqa_softmax(blk, keys, vals), qb)
    o = jnp.moveaxis(o, 0, 1).reshape(B, T, ATT_Q_W)
    if not need_ctx:
        return o, None
    qc = rmsnorm(q_c.reshape(B, L, ATT_HEADS, HEAD_DIM), q_norm_g).reshape(B, L, ATT_KV_HEADS, grp, HEAD_DIM)
    o_c = gqa_softmax(qc, k_c, v_c).reshape(B, L, ATT_Q_W)
    return o, o_c


def short_conv(x, w):
    C = x.shape[-1]
    left = CONV_K // 2
    right = CONV_K - 1 - left
    y = lax.conv_general_dilated(x, w[:, None, :], (1,), [(left, right)],
                                 dimension_numbers=('NWC', 'WIO', 'NWC'), feature_group_count=C)
    return jax.nn.silu(y)


def gdn_inputs(q, k, v, a, b, conv_w, a_log, dt_bias):
    B, T, _ = q.shape
    qkv = short_conv(jnp.concatenate([q, k, v], axis=-1), conv_w)
    q, k, v = jnp.split(qkv, [GDN_QK_W, 2 * GDN_QK_W], axis=-1)
    q = l2norm(jnp.swapaxes(q.reshape(B, T, GDN_HEADS, GDN_DK), 1, 2))
    k = l2norm(jnp.swapaxes(k.reshape(B, T, GDN_HEADS, GDN_DK), 1, 2))
    v = jnp.swapaxes(v.reshape(B, T, GDN_HEADS, GDN_DV), 1, 2).astype(F32)
    a = a.astype(F32).reshape(B, T, 2, GDN_HEADS)
    g = -jnp.exp(a_log.astype(F32)) * jax.nn.softplus(a + dt_bias.astype(F32))
    beta = jax.nn.sigmoid(b.astype(F32).reshape(B, T, 2, GDN_HEADS))
    return q, k, v, jnp.transpose(g, (2, 0, 3, 1)), jnp.transpose(beta, (2, 0, 3, 1))


def gated_delta_chunked(q, k, v, g, beta, s0):
    B, H, T, dk = q.shape
    dv = v.shape[-1]
    n = T // CHUNK

    def chunks(t):
        return t.astype(F32).reshape(B, H, n, CHUNK, *t.shape[3:])

    q, k, v, g, beta = (chunks(t) for t in (q, k, v, g, beta))
    q = q * (dk ** -0.5)
    gcum = jnp.cumsum(g, axis=-1)
    tri = jnp.tril(jnp.ones((CHUNK, CHUNK), bool))
    strict = jnp.tril(jnp.ones((CHUNK, CHUNK), bool), -1)
    diff = gcum[..., :, None] - gcum[..., None, :]
    decay = jnp.where(tri, jnp.exp(jnp.where(tri, diff, 0.0)), 0.0)
    kb = k * beta[..., None]
    m = jnp.where(strict, jnp.einsum('bhnid,bhnjd->bhnij', kb, k) * decay, 0.0)
    a = m + jnp.eye(CHUNK, dtype=F32)
    u = lax.linalg.triangular_solve(a, v * beta[..., None], left_side=True, lower=True, unit_diagonal=True)
    w = lax.linalg.triangular_solve(a, kb * jnp.exp(gcum)[..., None], left_side=True, lower=True, unit_diagonal=True)
    attn = jnp.einsum('bhnid,bhnjd->bhnij', q, k) * decay
    q_dec = q * jnp.exp(gcum)[..., None]
    g_last = gcum[..., -1]
    k_dec = k * jnp.exp(g_last[..., None] - gcum)[..., None]
    xs = tuple(jnp.moveaxis(t, 2, 0) for t in (u, w, q_dec, k_dec, attn, g_last))

    def step(s, inp):
        u_i, w_i, qd_i, kd_i, at_i, gl_i = inp
        v_new = u_i - jnp.einsum('bhcd,bhde->bhce', w_i, s)
        o_i = jnp.einsum('bhcd,bhde->bhce', qd_i, s) + jnp.einsum('bhij,bhje->bhie', at_i, v_new)
        s = s * jnp.exp(gl_i)[..., None, None] + jnp.einsum('bhcd,bhce->bhde', kd_i, v_new)
        return s, o_i

    s, o = lax.scan(step, s0.astype(F32), xs)
    o = jnp.moveaxis(o, 0, 2).reshape(B, H, T, dv)
    return o, s


def gated_head_norm(o, z, norm_g):
    B, H, T, dv = o.shape
    o = jnp.swapaxes(o, 1, 2)
    y = rmsnorm(o, norm_g) * jax.nn.silu(z.astype(F32).reshape(B, T, H, dv))
    return y.reshape(B, T, GDN_V_W).astype(z.dtype)


def gdn_mixer(q, k, v, z, a, b, q_c, k_c, v_c, z_c, a_c, b_c, conv_w, a_log, dt_bias, norm_g, need_ctx):
    lat = gdn_inputs(q, k, v, a, b, conv_w, a_log, dt_bias)
    con = gdn_inputs(q_c, k_c, v_c, a_c, b_c, conv_w, a_log, dt_bias)
    B = q.shape[0]
    o = 0.0
    o_c = 0.0
    for d in range(2):
        flip = (lambda t: jnp.flip(t, axis=2)) if d == 1 else (lambda t: t)
        s0 = jnp.zeros((B, GDN_HEADS, GDN_DK, GDN_DV), F32)
        oc_d, s_ctx = gated_delta_chunked(flip(con[0]), flip(con[1]), flip(con[2]), flip(con[3][d]), flip(con[4][d]), s0)
        ol_d, _ = gated_delta_chunked(flip(lat[0]), flip(lat[1]), flip(lat[2]), flip(lat[3][d]), flip(lat[4][d]), s_ctx)
        o = o + flip(ol_d)
        if need_ctx:
            o_c = o_c + flip(oc_d)
    y = gated_head_norm(o, z, norm_g)
    y_c = gated_head_norm(o_c, z_c, norm_g) if need_ctx else None
    return y, y_c


def merge_branches(branches, gate_logits, w_branch, w_out):
    gates = jnp.split(gate_logits, N_BRANCH, axis=-1)
    acc = jax.nn.sigmoid(gates[0]) * (branches[0] @ w_branch[0])
    for i in range(1, N_BRANCH):
        acc = acc + jax.nn.sigmoid(gates[i]) * (branches[i] @ w_branch[i])
    return acc @ w_out


def hybrid_mixer(p, p_c, cos, sin, pool_w, pool_scale, q_norm_g, k_norm_g, conv_w, a_log, dt_bias,
                 gdn_norm_g, w_branch, w_out, need_ctx):
    (u, aq, ak, av, gq, gk, gv, gz, ga, gb, gate) = split_cols(p)
    (u_c, aq_c, ak_c, av_c, gq_c, gk_c, gv_c, gz_c, ga_c, gb_c, gate_c) = split_cols(p_c)
    pool = pool_mixer(u, pool_w, pool_scale)
    att, att_c = attention_mixer(aq, ak, av, aq_c, ak_c, av_c, q_norm_g, k_norm_g, cos, sin, need_ctx)
    gdn, gdn_c = gdn_mixer(gq, gk, gv, gz, ga, gb, gq_c, gk_c, gv_c, gz_c, ga_c, gb_c,
                           conv_w, a_log, dt_bias, gdn_norm_g, need_ctx)
    y = merge_branches((pool, att, gdn), gate, w_branch, w_out)
    if not need_ctx:
        return y, None
    pool_c = pool_mixer(u_c, pool_w, pool_scale)
    y_c = merge_branches((pool_c, att_c, gdn_c), gate_c, w_branch, w_out)
    return y, y_c


def expert_choice_ffn(h, w_router, w_gate, w_up, w_down):
    B, T, D = h.shape
    cap = CAPACITY_FACTOR * T // N_EXPERTS
    logits = jnp.einsum('btd,de->bte', h, w_router, preferred_element_type=F32)
    aff = jax.nn.softmax(logits, axis=-1)
    top_val, top_idx = lax.top_k(jnp.swapaxes(aff, 1, 2), cap)
    xe = jax.vmap(lambda hb, ib: hb[ib])(h, top_idx)
    hid = jax.nn.silu(jnp.einsum('becd,edf->becf', xe, w_gate)) * jnp.einsum('becd,edf->becf', xe, w_up)
    ye = jnp.einsum('becf,efd->becd', hid, w_down) * top_val[..., None].astype(h.dtype)
    return jax.vmap(lambda ib, yb: jnp.zeros((T, D), yb.dtype).at[ib.reshape(-1)].add(yb.reshape(-1, D)))(top_idx, ye)


def setup_inputs(seed: int = 0) -> dict:
    key = jax.random.key(seed)
    ks = jax.random.split(key, 24)
    L, D = DEPTH, D_MODEL

    def nrm(k, shape, s):
        return jax.random.normal(k, shape, F32) * s

    dt = jnp.exp(jax.random.uniform(ks[15], (L, 2, GDN_HEADS), F32, math.log(1e-3), math.log(1e-1)))
    return {
        'x': nrm(ks[0], (BATCH, SEQ, D), 1.0),
        'c': nrm(ks[1], (BATCH, D), 1.0),
        'ctx': nrm(ks[2], (BATCH, CTX_LEN, D), 1.0),
        'c_ctx': nrm(ks[3], (D,), 1.0),
        'w_mod': nrm(ks[4], (L, D, 6 * D), 0.5 * D ** -0.5),
        'b_mod': nrm(ks[5], (L, 6 * D), 0.02),
        'norm1_g': 1.0 + nrm(ks[6], (L, D), 0.02),
        'norm2_g': 1.0 + nrm(ks[7], (L, D), 0.02),
        'w_in': nrm(ks[8], (L, D, N_IN), D ** -0.5),
        'pool_w': nrm(ks[9], (L, len(POOL_WINDOWS), POOL_GROUP, POOL_GROUP), POOL_GROUP ** -0.5),
        'pool_scale': 1.0 + nrm(ks[10], (L, POOL_WIDTH), 0.02),
        'q_norm_g': 1.0 + nrm(ks[11], (L, HEAD_DIM), 0.02),
        'k_norm_g': 1.0 + nrm(ks[12], (L, HEAD_DIM), 0.02),
        'conv_w': nrm(ks[13], (L, CONV_K, 2 * GDN_QK_W + GDN_V_W), CONV_K ** -0.5),
        'a_log': jnp.log(jax.random.uniform(ks[14], (L, 2, GDN_HEADS), F32, 1.0, 16.0)),
        'dt_bias': dt + jnp.log(-jnp.expm1(-dt)),
        'gdn_norm_g': 1.0 + nrm(ks[16], (L, GDN_DV), 0.02),
        'w_branch': nrm(ks[17], (L, N_BRANCH, BRANCH_W, D), BRANCH_W ** -0.5),
        'w_out': nrm(ks[18], (L, D, D), D ** -0.5),
        'w_router': nrm(ks[19], (L, D, N_EXPERTS), D ** -0.5),
        'w_e_gate': nrm(ks[20], (L, N_EXPERTS, D, EXPERT_FF), D ** -0.5),
        'w_e_up': nrm(ks[21], (L, N_EXPERTS, D, EXPERT_FF), D ** -0.5),
        'w_e_down': nrm(ks[22], (L, N_EXPERTS, EXPERT_FF, D), EXPERT_FF ** -0.5),
        'final_g': 1.0 + nrm(ks[23], (D,), 0.02),
    }


def reference(x, c, ctx, c_ctx, w_mod, b_mod, norm1_g, norm2_g, w_in, pool_w, pool_scale, q_norm_g,
              k_norm_g, conv_w, a_log, dt_bias, gdn_norm_g, w_branch, w_out, w_router, w_e_gate,
              w_e_up, w_e_down, final_g):
    B, T, D = x.shape
    rows = T // GRID_W
    row = jnp.repeat(jnp.arange(rows), GRID_W)
    col = jnp.tile(jnp.arange(GRID_W), rows)
    cos, sin = axial_rope(row, col)
    c_act = jax.nn.silu(c)
    cc_act = jax.nn.silu(c_ctx)
    for l in range(DEPTH):
        need_ctx = l < DEPTH - 1
        mod = (c_act @ w_mod[l] + b_mod[l])[:, None, :]
        mod_c = cc_act @ w_mod[l] + b_mod[l]
        sh1, sc1, ga1, sh2, sc2, ga2 = jnp.split(mod, 6, axis=-1)
        sh1c, sc1c, ga1c, sh2c, sc2c, ga2c = jnp.split(mod_c, 6, axis=-1)
        h = modulate(rmsnorm(x, norm1_g[l]), sh1, sc1)
        h_c = modulate(rmsnorm(ctx, norm1_g[l]), sh1c, sc1c)
        y, y_c = hybrid_mixer(h @ w_in[l], h_c @ w_in[l], cos, sin, pool_w[l], pool_scale[l], q_norm_g[l],
                              k_norm_g[l], conv_w[l], a_log[l], dt_bias[l], gdn_norm_g[l], w_branch[l],
                              w_out[l], need_ctx)
        x = x + ga1 * y
        h2 = modulate(rmsnorm(x, norm2_g[l]), sh2, sc2)
        x = x + ga2 * expert_choice_ffn(h2, w_router[l], w_e_gate[l], w_e_up[l], w_e_down[l])
        if need_ctx:
            ctx = ctx + ga1c * y_c
            h2_c = modulate(rmsnorm(ctx, norm2_g[l]), sh2c, sc2c)
            ctx = ctx + ga2c * expert_choice_ffn(h2_c, w_router[l], w_e_gate[l], w_e_up[l], w_e_down[l])
    return rmsnorm(x, final_g)
```

```python
import functools
import math

import jax
import jax.numpy as jnp
from jax import lax
from jax.experimental import pallas as pl
from jax.experimental.pallas import tpu as pltpu

F32 = jnp.float32
BF16 = jnp.bfloat16
I32 = jnp.int32

EPS = 1e-6
HEAD_DIM = 128
ATT_KV_HEADS = 2
GRID_W = 64
ROPE_THETA = 10000.0
CHUNK = 64
CONV_K = 4
N_BRANCH = 3
N_POOL = 4
CAPACITY_FACTOR = 2
LANES = 128
NEG = -1e30
HI = lax.Precision.HIGHEST


def _cparams(sem, vmem_mb=48):
    return pltpu.CompilerParams(dimension_semantics=sem, vmem_limit_bytes=vmem_mb * 1024 * 1024)


def _pick(n, cap, mult):
    best = None
    for t in range(mult, min(n, cap) + 1, mult):
        if n % t == 0:
            best = t
    assert best is not None, (n, cap, mult)
    return best


def _iota(shape, dim):
    return lax.broadcasted_iota(I32, shape, dim)


def _silu(x):
    return x / (1.0 + jnp.exp(-x))


def _sigmoid(x):
    return 1.0 / (1.0 + jnp.exp(-x))


def _mod_kernel(c_ref, w_ref, b_ref, o_ref):
    a = _silu(c_ref[...])
    o_ref[...] = jnp.dot(a, w_ref[...], preferred_element_type=F32, precision=HI) + b_ref[...]


def mod_call(cstack, w_mod, b_mod):
    r, d = cstack.shape
    n = w_mod.shape[1]
    tn = _pick(n, 1024, LANES)
    return pl.pallas_call(
        _mod_kernel,
        grid=(n // tn,),
        in_specs=[pl.BlockSpec((r, d), lambda j: (0, 0)),
                  pl.BlockSpec((d, tn), lambda j: (0, j)),
                  pl.BlockSpec((1, tn), lambda j: (0, j))],
        out_specs=pl.BlockSpec((r, tn), lambda j: (0, j)),
        out_shape=jax.ShapeDtypeStruct((r, n), F32),
        compiler_params=_cparams(("parallel",)),
        name="adaln_mod",
    )(cstack, w_mod, b_mod.reshape(1, n))


def _norm_kernel(x_ref, g_ref, sh_ref, sc_ref, o_ref, *, rb, nsub):
    for s in range(nsub):
        x = x_ref[s * rb:(s + 1) * rb, :]
        y = x * lax.rsqrt(jnp.mean(x * x, axis=-1, keepdims=True) + EPS) * g_ref[...]
        h = y * (1.0 + sc_ref[s]) + sh_ref[s]
        o_ref[s * rb:(s + 1) * rb, :] = h.astype(o_ref.dtype)


def _norm_router_kernel(x_ref, g_ref, sh_ref, sc_ref, wr_ref, o_ref, aff_ref, *, rb, nsub, n_exp):
    for s in range(nsub):
        x = x_ref[s * rb:(s + 1) * rb, :]
        y = x * lax.rsqrt(jnp.mean(x * x, axis=-1, keepdims=True) + EPS) * g_ref[...]
        h = y * (1.0 + sc_ref[s]) + sh_ref[s]
        o_ref[s * rb:(s + 1) * rb, :] = h.astype(o_ref.dtype)
        logits = jnp.dot(h, wr_ref[...], preferred_element_type=F32, precision=HI)
        lane = _iota(logits.shape, 1)
        logits = jnp.where(lane < n_exp, logits, NEG)
        m = jnp.max(logits, axis=-1, keepdims=True)
        e = jnp.exp(logits - m)
        aff_ref[s * rb:(s + 1) * rb, :] = e / jnp.sum(e, axis=-1, keepdims=True)


def norm_call(x2, g, sh_tab, sc_tab, rb, w_router=None, n_exp=0):
    m, d = x2.shape
    nblk = m // rb
    nsub = _pick(nblk, max(1, 1024 // rb), 1)
    tm = nsub * rb
    in_specs = [pl.BlockSpec((tm, d), lambda i: (i, 0)),
                pl.BlockSpec((1, d), lambda i: (0, 0)),
                pl.BlockSpec((nsub, 1, d), lambda i: (i, 0, 0)),
                pl.BlockSpec((nsub, 1, d), lambda i: (i, 0, 0))]
    if w_router is None:
        return pl.pallas_call(
            functools.partial(_norm_kernel, rb=rb, nsub=nsub),
            grid=(m // tm,), in_specs=in_specs,
            out_specs=pl.BlockSpec((tm, d), lambda i: (i, 0)),
            out_shape=jax.ShapeDtypeStruct((m, d), BF16),
            compiler_params=_cparams(("parallel",)),
            name="norm_mod",
        )(x2, g.reshape(1, d), sh_tab, sc_tab)
    return pl.pallas_call(
        functools.partial(_norm_router_kernel, rb=rb, nsub=nsub, n_exp=n_exp),
        grid=(m // tm,),
        in_specs=in_specs + [pl.BlockSpec((d, LANES), lambda i: (0, 0))],
        out_specs=[pl.BlockSpec((tm, d), lambda i: (i, 0)),
                   pl.BlockSpec((tm, LANES), lambda i: (i, 0))],
        out_shape=[jax.ShapeDtypeStruct((m, d), BF16), jax.ShapeDtypeStruct((m, LANES), F32)],
        compiler_params=_cparams(("parallel",)),
        name="norm_mod_router",
    )(x2, g.reshape(1, d), sh_tab, sc_tab, w_router)


def _final_norm_kernel(x_ref, g_ref, o_ref):
    x = x_ref[0]
    o_ref[0] = x * lax.rsqrt(jnp.mean(x * x, axis=-1, keepdims=True) + EPS) * g_ref[...]


def final_norm_call(xall, g, l_ctx, t_lat):
    b, tall, d = xall.shape
    tr = _pick(math.gcd(l_ctx, t_lat), 512, 8)
    off = l_ctx // tr
    return pl.pallas_call(
        _final_norm_kernel,
        grid=(b, t_lat // tr),
        in_specs=[pl.BlockSpec((1, tr, d), lambda bi, i: (bi, i + off, 0)),
                  pl.BlockSpec((1, d), lambda bi, i: (0, 0))],
        out_specs=pl.BlockSpec((1, tr, d), lambda bi, i: (bi, i, 0)),
        out_shape=jax.ShapeDtypeStruct((b, t_lat, d), F32),
        compiler_params=_cparams(("parallel", "parallel")),
        name="final_norm",
    )(xall, g.reshape(1, d))


def _matmul_kernel(a_ref, w_ref, o_ref):
    o_ref[...] = jnp.dot(a_ref[...], w_ref[...], preferred_element_type=F32).astype(o_ref.dtype)


def matmul_call(a, w, out_dtype, tn_cap=512, name="matmul"):
    m, k = a.shape
    n = w.shape[1]
    tm = _pick(m, 1024, 16)
    tn = _pick(n, tn_cap, LANES)
    return pl.pallas_call(
        _matmul_kernel,
        grid=(m // tm, n // tn),
        in_specs=[pl.BlockSpec((tm, k), lambda i, j: (i, 0)),
                  pl.BlockSpec((k, tn), lambda i, j: (0, j))],
        out_specs=pl.BlockSpec((tm, tn), lambda i, j: (i, j)),
        out_shape=jax.ShapeDtypeStruct((m, n), out_dtype),
        compiler_params=_cparams(("parallel", "arbitrary")),
        name=name,
    )(a, w)


def _pool_kernel(u_ref, pw_ref, ps_ref, o_ref, *, l_ctx, t_lat, tr):
    half = jnp.left_shift(1, pl.program_id(1))

    def tile(r0, s0, kext, seg_lo, seg_hi):
        src = u_ref[0, pl.ds(s0, kext), :]
        t = r0 + _iota((tr, 1), 0)
        c = s0 + _iota((1, kext), 1)
        lo = jnp.maximum(t - half, seg_lo)
        hi = jnp.minimum(t + half, seg_hi)
        band = jnp.where(c >= lo, jnp.where(c < hi, 1.0, 0.0), 0.0).astype(BF16)
        win = jnp.dot(band, src, preferred_element_type=F32)
        cnt = (hi - lo).astype(F32)
        y = win / cnt - u_ref[0, pl.ds(r0, tr), :].astype(F32)
        z = jnp.dot(y.astype(BF16), pw_ref[0], preferred_element_type=F32) * ps_ref[0]
        o_ref[0, pl.ds(r0, tr), :] = z.astype(o_ref.dtype)

    kc = min(tr + 256, l_ctx)
    for i in range(l_ctx // tr):
        r0 = i * tr
        s0 = min(max(r0 - 128, 0), l_ctx - kc)
        tile(r0, s0, kc, 0, l_ctx)

    kl = min(tr + 256, t_lat)
    tall = l_ctx + t_lat

    def body(i, carry):
        r0 = pl.multiple_of(l_ctx + i * tr, tr)
        s0 = pl.multiple_of(jnp.clip(r0 - 128, l_ctx, tall - kl), 128)
        tile(r0, s0, kl, l_ctx, tall)
        return carry

    lax.fori_loop(0, t_lat // tr, body, 0)


def pool_call(p3, pool_w, pool_scale, l_ctx, t_lat):
    b, tall, _ = p3.shape
    g = pool_w.shape[1]
    tr = _pick(math.gcd(l_ctx, t_lat), 256, 128)
    return pl.pallas_call(
        functools.partial(_pool_kernel, l_ctx=l_ctx, t_lat=t_lat, tr=tr),
        grid=(b, N_POOL),
        in_specs=[pl.BlockSpec((1, tall, g), lambda bi, gi: (bi, 0, gi)),
                  pl.BlockSpec((1, g, g), lambda bi, gi: (gi, 0, 0)),
                  pl.BlockSpec((1, 1, g), lambda bi, gi: (gi, 0, 0))],
        out_specs=pl.BlockSpec((1, tall, g), lambda bi, gi: (bi, 0, gi)),
        out_shape=jax.ShapeDtypeStruct((b, tall, N_POOL * g), BF16),
        compiler_params=_cparams(("parallel", "parallel")),
        name="pool_mixer",
    )(p3, pool_w, pool_scale.reshape(N_POOL, 1, g))


def _rms_rope(x, g, cos, sin):
    xn = x * lax.rsqrt(jnp.mean(x * x, axis=-1, keepdims=True) + EPS) * g
    return xn * cos + pltpu.roll(xn, HEAD_DIM // 2, 1) * sin


def _attn_kernel(q_ref, k_ref, v_ref, cq_ref, sq_ref, ck_ref, sk_ref, qg_ref, kg_ref, o_ref, k_scr,
                 *, grp, tq, l_ctx):
    i = pl.program_id(2)
    tall = k_ref.shape[1]

    @pl.when(i == 0)
    def _():
        k_scr[...] = _rms_rope(k_ref[0].astype(F32), kg_ref[...], ck_ref[...], sk_ref[...]).astype(BF16)

    scale = HEAD_DIM ** -0.5
    qs = []
    for h in range(grp):
        q = q_ref[0, :, h * HEAD_DIM:(h + 1) * HEAD_DIM].astype(F32)
        qs.append((_rms_rope(q, qg_ref[...], cq_ref[...], sq_ref[...]) * scale).astype(BF16))
    qcat = jnp.concatenate(qs, axis=0)
    s = lax.dot_general(qcat, k_scr[...], (((1,), (1,)), ((), ())), preferred_element_type=F32)
    limit = jnp.where(i * tq < l_ctx, l_ctx, tall)
    s = jnp.where(_iota((1, tall), 1) < limit, s, NEG)
    m = jnp.max(s, axis=-1, keepdims=True)
    p = jnp.exp(s - m)
    den = jnp.sum(p, axis=-1, keepdims=True)
    o = jnp.dot(p.astype(BF16), v_ref[0], preferred_element_type=F32) / den
    for h in range(grp):
        o_ref[0, :, h * HEAD_DIM:(h + 1) * HEAD_DIM] = o[h * tq:(h + 1) * tq].astype(o_ref.dtype)


def attn_call(p3, cos_t, sin_t, q_g, k_g, offs, l_ctx, t_lat, heads):
    b, tall, _ = p3.shape
    grp = heads // ATT_KV_HEADS
    qw = grp * HEAD_DIM
    tq = _pick(math.gcd(l_ctx, t_lat), 128, 16)
    qb, kb, vb = offs["aq"] // qw, offs["ak"] // HEAD_DIM, offs["av"] // HEAD_DIM
    return pl.pallas_call(
        functools.partial(_attn_kernel, grp=grp, tq=tq, l_ctx=l_ctx),
        grid=(b, ATT_KV_HEADS, tall // tq),
        in_specs=[pl.BlockSpec((1, tq, qw), lambda bi, kv, i: (bi, i, qb + kv)),
                  pl.BlockSpec((1, tall, HEAD_DIM), lambda bi, kv, i: (bi, 0, kb + kv)),
                  pl.BlockSpec((1, tall, HEAD_DIM), lambda bi, kv, i: (bi, 0, vb + kv)),
                  pl.BlockSpec((tq, HEAD_DIM), lambda bi, kv, i: (i, 0)),
                  pl.BlockSpec((tq, HEAD_DIM), lambda bi, kv, i: (i, 0)),
                  pl.BlockSpec((tall, HEAD_DIM), lambda bi, kv, i: (0, 0)),
                  pl.BlockSpec((tall, HEAD_DIM), lambda bi, kv, i: (0, 0)),
                  pl.BlockSpec((1, HEAD_DIM), lambda bi, kv, i: (0, 0)),
                  pl.BlockSpec((1, HEAD_DIM), lambda bi, kv, i: (0, 0))],
        out_specs=pl.BlockSpec((1, tq, qw), lambda bi, kv, i: (bi, i, kv)),
        out_shape=jax.ShapeDtypeStruct((b, tall, heads * HEAD_DIM), BF16),
        scratch_shapes=[pltpu.VMEM((tall, HEAD_DIM), BF16)],
        compiler_params=_cparams(("parallel", "parallel", "arbitrary"), 56),
        name="gqa_attention",
    )(p3, p3, p3, cos_t, sin_t, cos_t, sin_t, q_g.reshape(1, HEAD_DIM), k_g.reshape(1, HEAD_DIM))


def _gdn_gate_kernel(ab_ref, alog_ref, dtb_ref, o_ref, *, n_gate):
    x = ab_ref[0]
    z = x + dtb_ref[...]
    softplus = jnp.maximum(z, 0.0) + jnp.log1p(jnp.exp(-jnp.abs(z)))
    g = -jnp.exp(alog_ref[...]) * softplus
    beta = _sigmoid(x)
    o_ref[0] = jnp.where(_iota(x.shape, 1) < n_gate, g, beta)


def gdn_gate_call(pab3, a_log, dt_bias):
    b, tall, w = pab3.shape
    n_gate = a_log.size
    pad = lambda v: jnp.pad(v.reshape(1, -1).astype(F32), ((0, 0), (0, w - n_gate)))
    return pl.pallas_call(
        functools.partial(_gdn_gate_kernel, n_gate=n_gate),
        grid=(b,),
        in_specs=[pl.BlockSpec((1, tall, w), lambda bi: (bi, 0, 0)),
                  pl.BlockSpec((1, w), lambda bi: (0, 0)),
                  pl.BlockSpec((1, w), lambda bi: (0, 0))],
        out_specs=pl.BlockSpec((1, tall, w), lambda bi: (bi, 0, 0)),
        out_shape=jax.ShapeDtypeStruct((b, tall, w), F32),
        compiler_params=_cparams(("parallel",)),
        name="gdn_gates",
    )(pab3, pad(a_log), pad(dt_bias))


def _gdn_prep_kernel(q_ref, k_ref, v_ref, wq_ref, wk_ref, wv_ref, qo_ref, ko_ref, vo_ref, *, l_ctx):
    tall = q_ref.shape[1]
    t = _iota((tall, 1), 0)
    seg_lo = jnp.where(t < l_ctx, 0, l_ctx)
    seg_hi = jnp.where(t < l_ctx, l_ctx, tall)
    left = CONV_K // 2
    masks = [jnp.where((t + (j - left) >= seg_lo) & (t + (j - left) < seg_hi), 1.0, 0.0) for j in range(CONV_K)]

    def conv_silu(x_ref, w_ref):
        x = x_ref[0].astype(F32)
        acc = jnp.zeros_like(x)
        for j in range(CONV_K):
            off = j - left
            xs = x if off == 0 else pltpu.roll(x, (-off) % tall, 0)
            acc = acc + xs * masks[j] * w_ref[j:j + 1, :]
        return _silu(acc)

    q = conv_silu(q_ref, wq_ref)
    k = conv_silu(k_ref, wk_ref)
    v = conv_silu(v_ref, wv_ref)
    q = q * lax.rsqrt(jnp.sum(q * q, axis=-1, keepdims=True) + EPS) * (HEAD_DIM ** -0.5)
    k = k * lax.rsqrt(jnp.sum(k * k, axis=-1, keepdims=True) + EPS)
    qo_ref[0] = q.astype(qo_ref.dtype)
    ko_ref[0] = k.astype(ko_ref.dtype)
    vo_ref[0] = v.astype(vo_ref.dtype)


def gdn_prep_call(p3, conv_w, offs, l_ctx, heads):
    b, tall, _ = p3.shape
    qb, kb, vb = (offs[n] // HEAD_DIM for n in ("gq", "gk", "gv"))
    hw = heads * HEAD_DIM
    spec = lambda base: pl.BlockSpec((1, tall, HEAD_DIM), lambda bi, h: (bi, 0, base + h))
    wspec = lambda base: pl.BlockSpec((CONV_K, HEAD_DIM), lambda bi, h: (0, base + h))
    out = jax.ShapeDtypeStruct((b, tall, hw), BF16)
    return pl.pallas_call(
        functools.partial(_gdn_prep_kernel, l_ctx=l_ctx),
        grid=(b, heads),
        in_specs=[spec(qb), spec(kb), spec(vb), wspec(0), wspec(heads), wspec(2 * heads)],
        out_specs=[spec(0), spec(0), spec(0)],
        out_shape=[out, out, out],
        compiler_params=_cparams(("parallel", "parallel")),
        name="gdn_prep",
    )(p3, p3, p3, conv_w, conv_w, conv_w)


def _gdn_scan_kernel(q_ref, k_ref, v_ref, gc_ref, gr_ref, o_ref, s_scr, *, heads):
    d = pl.program_id(1)
    c = pl.program_id(2)
    ck = CHUNK

    @pl.when(c == 0)
    def _():
        s_scr[...] = jnp.zeros_like(s_scr)

    rev = d == 1
    ri = _iota((ck, ck), 0)
    ci = _iota((ck, ck), 1)
    a_idx = jnp.where(rev, ci, ri)
    b_idx = jnp.where(rev, ri, ci)
    incl = a_idx >= b_idx
    strict = a_idx > b_idx
    tmat = jnp.where(incl, 1.0, 0.0)

    gcol = gc_ref[0, 0, 0]
    grow = gr_ref[0, 0, 0]
    gcum_c = jnp.dot(tmat, gcol, preferred_element_type=F32, precision=HI)
    gcum_r = lax.dot_general(grow, tmat, (((1,), (1,)), ((), ())), preferred_element_type=F32, precision=HI)
    glast = jnp.sum(gcol, axis=0, keepdims=True)

    xor = jnp.bitwise_xor(ri, ci)
    level_masks = []
    sz = 2
    while sz < ck:
        level_masks.append((xor >= sz) & (xor < 2 * sz))
        sz *= 2
    base_mask = xor < 2
    eye = jnp.where(ri == ci, 1.0, 0.0)

    for h in range(heads):
        sl = slice(h * HEAD_DIM, (h + 1) * HEAD_DIM)
        q = q_ref[0, :, sl]
        k = k_ref[0, :, sl]
        v = v_ref[0, :, sl]
        gc_c = gcum_c[:, h:h + 1]
        gc_r = gcum_r[h:h + 1, :]
        beta_c = gcol[:, heads + h:heads + h + 1]
        beta_r = grow[heads + h:heads + h + 1, :]
        gl = glast[:, h:h + 1]

        decay = jnp.where(incl, jnp.exp(jnp.where(incl, gc_c - gc_r, 0.0)), 0.0)
        kq = jnp.concatenate([k, q], axis=0)
        prod = lax.dot_general(kq, k, (((1,), (1,)), ((), ())), preferred_element_type=F32)
        kk = prod[:ck]
        qk = prod[ck:]
        mm = jnp.where(strict, kk * beta_c * decay, 0.0)

        x = eye - jnp.where(base_mask, mm, 0.0)
        for lm in level_masks:
            a_off = jnp.where(lm, mm, 0.0).astype(BF16)
            t1 = jnp.dot(a_off, x.astype(BF16), preferred_element_type=F32)
            x = x - jnp.dot(x.astype(BF16), t1.astype(BF16), preferred_element_type=F32)

        eg = jnp.exp(gc_c)
        kf = k.astype(F32)
        rhs = jnp.concatenate([v, (kf * eg).astype(BF16)], axis=1)
        uw = jnp.dot((x * beta_r).astype(BF16), rhs, preferred_element_type=F32)
        u = uw[:, :HEAD_DIM]
        w = uw[:, HEAD_DIM:]

        s = s_scr[h]
        wq = jnp.concatenate([w.astype(BF16), (q.astype(F32) * eg).astype(BF16)], axis=0)
        ws_qs = jnp.dot(wq, s.astype(BF16), preferred_element_type=F32)
        v_new = (u - ws_qs[:ck]).astype(BF16)
        attn = jnp.where(incl, qk * decay, 0.0).astype(BF16)
        o = ws_qs[ck:] + jnp.dot(attn, v_new, preferred_element_type=F32)
        kd = (kf * jnp.exp(gl - gc_c)).astype(BF16)
        s_scr[h] = s * jnp.exp(gl) + lax.dot_general(kd, v_new, (((0,), (0,)), ((), ())),
                                                     preferred_element_type=F32)
        o_ref[0, 0, :, sl] = o.astype(o_ref.dtype)


def gdn_scan_call(qn, kn, vn, gcol, grow, l_ctx, heads):
    b, tall, hw = qn.shape
    nc = tall // CHUNK
    nctx = l_ctx // CHUNK

    def cidx(d, c):
        bwd = jnp.where(c < nctx, nctx - 1 - c, nc - 1 - (c - nctx))
        return jnp.where(d == 0, c, bwd)

    tok = pl.BlockSpec((1, CHUNK, hw), lambda bi, d, c: (bi, cidx(d, c), 0))
    return pl.pallas_call(
        functools.partial(_gdn_scan_kernel, heads=heads),
        grid=(b, 2, nc),
        in_specs=[tok, tok, tok,
                  pl.BlockSpec((1, 1, 1, CHUNK, 2 * heads), lambda bi, d, c: (d, bi, cidx(d, c), 0, 0)),
                  pl.BlockSpec((1, 1, 1, 2 * heads, CHUNK), lambda bi, d, c: (d, bi, cidx(d, c), 0, 0))],
        out_specs=pl.BlockSpec((1, 1, CHUNK, hw), lambda bi, d, c: (d, bi, cidx(d, c), 0)),
        out_shape=jax.ShapeDtypeStruct((2, b, tall, hw), BF16),
        scratch_shapes=[pltpu.VMEM((heads, HEAD_DIM, HEAD_DIM), F32)],
        compiler_params=_cparams(("parallel", "parallel", "arbitrary")),
        name="gdn_scan",
    )(qn, kn, vn, gcol, grow)


def _gdn_out_kernel(of_ref, ob_ref, z_ref, g_ref, o_ref, *, heads):
    for h in range(heads):
        sl = slice(h * HEAD_DIM, (h + 1) * HEAD_DIM)
        o = of_ref[0, 0, :, sl].astype(F32) + ob_ref[0, 0, :, sl].astype(F32)
        y = o * lax.rsqrt(jnp.mean(o * o, axis=-1, keepdims=True) + EPS) * g_ref[...]
        o_ref[0, :, sl] = (y * _silu(z_ref[0, :, sl].astype(F32))).astype(o_ref.dtype)


def gdn_out_call(o2, p3, norm_g, offs, heads):
    _, b, tall, hw = o2.shape
    tr = _pick(tall, 1088, 16)
    zb = offs["gz"] // hw
    return pl.pallas_call(
        functools.partial(_gdn_out_kernel, heads=heads),
        grid=(b, tall // tr),
        in_specs=[pl.BlockSpec((1, 1, tr, hw), lambda bi, i: (0, bi, i, 0)),
                  pl.BlockSpec((1, 1, tr, hw), lambda bi, i: (1, bi, i, 0)),
                  pl.BlockSpec((1, tr, hw), lambda bi, i: (bi, i, zb)),
                  pl.BlockSpec((1, HEAD_DIM), lambda bi, i: (0, 0))],
        out_specs=pl.BlockSpec((1, tr, hw), lambda bi, i: (bi, i, 0)),
        out_shape=jax.ShapeDtypeStruct((b, tall, hw), BF16),
        compiler_params=_cparams(("parallel", "parallel")),
        name="gdn_out_norm",
    )(o2, o2, p3, norm_g.reshape(1, HEAD_DIM))


def _merge_kernel(b0_ref, b1_ref, b2_ref, g0_ref, g1_ref, g2_ref, w0_ref, w1_ref, w2_ref, o_ref):
    acc = _sigmoid(g0_ref[...].astype(F32)) * jnp.dot(b0_ref[...], w0_ref[0], preferred_element_type=F32)
    acc += _sigmoid(g1_ref[...].astype(F32)) * jnp.dot(b1_ref[...], w1_ref[0], preferred_element_type=F32)
    acc += _sigmoid(g2_ref[...].astype(F32)) * jnp.dot(b2_ref[...], w2_ref[0], preferred_element_type=F32)
    o_ref[...] = acc.astype(o_ref.dtype)


def merge_call(branches, p2, w_branch, gate_off):
    m, bw = branches[0].shape
    d = w_branch.shape[2]
    tm = _pick(m, 1024, 16)
    tn = _pick(d, 512, LANES)
    gbase = [(gate_off + i * d) // tn for i in range(N_BRANCH)]
    bspec = pl.BlockSpec((tm, bw), lambda i, j: (i, 0))
    gspec = lambda base: pl.BlockSpec((tm, tn), lambda i, j: (i, base + j))
    wspec = lambda br: pl.BlockSpec((1, bw, tn), lambda i, j: (br, 0, j))
    return pl.pallas_call(
        _merge_kernel,
        grid=(m // tm, d // tn),
        in_specs=[bspec, bspec, bspec, gspec(gbase[0]), gspec(gbase[1]), gspec(gbase[2]),
                  wspec(0), wspec(1), wspec(2)],
        out_specs=pl.BlockSpec((tm, tn), lambda i, j: (i, j)),
        out_shape=jax.ShapeDtypeStruct((m, d), BF16),
        compiler_params=_cparams(("parallel", "arbitrary")),
        name="branch_merge",
    )(*branches, p2, p2, p2, w_branch, w_branch, w_branch)


def _outproj_kernel(a_ref, w_ref, x_ref, g_ref, o_ref, *, rb, nsub):
    y = jnp.dot(a_ref[...], w_ref[...], preferred_element_type=F32)
    for s in range(nsub):
        rows = slice(s * rb, (s + 1) * rb)
        o_ref[rows, :] = x_ref[rows, :] + g_ref[s] * y[rows, :]


def outproj_call(acc, w_out, x2, gate_tab, rb):
    m, k = acc.shape
    d = w_out.shape[1]
    nblk = m // rb
    nsub = _pick(nblk, max(1, 1024 // rb), 1)
    tm = nsub * rb
    tn = _pick(d, 512, LANES)
    return pl.pallas_call(
        functools.partial(_outproj_kernel, rb=rb, nsub=nsub),
        grid=(m // tm, d // tn),
        in_specs=[pl.BlockSpec((tm, k), lambda i, j: (i, 0)),
                  pl.BlockSpec((k, tn), lambda i, j: (0, j)),
                  pl.BlockSpec((tm, tn), lambda i, j: (i, j)),
                  pl.BlockSpec((nsub, 1, tn), lambda i, j: (i, 0, j))],
        out_specs=pl.BlockSpec((tm, tn), lambda i, j: (i, j)),
        out_shape=jax.ShapeDtypeStruct((m, d), F32),
        compiler_params=_cparams(("parallel", "arbitrary")),
        name="out_proj_residual",
    )(acc, w_out, x2, gate_tab)


def _route_kernel(aff_ref, pos_ref, eq_scr, sel_scr, *, segs, tr):
    tri = jnp.where(_iota((tr, tr), 0) >= _iota((tr, tr), 1), 1.0, 0.0).astype(BF16)

    def prefix(src_scr, dst_ref, start, n, fn):
        def body(i, carry):
            r0 = pl.multiple_of(start + i * tr, tr)
            blk = src_scr[pl.ds(r0, tr), :]
            incl = jnp.dot(tri, blk.astype(BF16), preferred_element_type=F32)
            dst_ref[pl.ds(r0, tr), :] = fn(carry + incl - blk, r0)
            return carry + incl[tr - 1:tr, :]
        lax.fori_loop(0, n // tr, body, jnp.zeros((1, LANES), F32))

    for (start, n, cap, base) in segs:
        bits = lax.bitcast_convert_type(aff_ref[0, start:start + n, :], I32)

        def search(it, pref):
            cand = pref | jnp.left_shift(1, 30 - it)
            cnt = jnp.sum(jnp.where(bits >= cand, 1.0, 0.0), axis=0, keepdims=True)
            return jnp.where(cnt >= cap, cand, pref)

        thr = lax.fori_loop(0, 31, search, jnp.zeros((1, LANES), I32))
        gt = jnp.where(bits > thr, 1.0, 0.0)
        eq = jnp.where(bits == thr, 1.0, 0.0)
        need = cap - jnp.sum(gt, axis=0, keepdims=True)
        eq_scr[start:start + n, :] = eq

        def pick(rank, r0):
            rows = pl.ds(r0, tr)
            b = lax.bitcast_convert_type(aff_ref[0, rows, :], I32)
            e = eq_scr[rows, :]
            return jnp.where(b > thr, 1.0, jnp.where(rank < need, e, 0.0))

        prefix(eq_scr, sel_scr, start, n, pick)

        def number(rank, r0):
            s = sel_scr[pl.ds(r0, tr), :]
            return jnp.where(s > 0.0, rank.astype(I32) + base, -1)

        prefix(sel_scr, pos_ref.at[0], start, n, number)


def route_call(aff3, segs):
    b, tall, w = aff3.shape
    tr = _pick(math.gcd(*[s[1] for s in segs]), 256, 16)
    return pl.pallas_call(
        functools.partial(_route_kernel, segs=segs, tr=tr),
        grid=(b,),
        in_specs=[pl.BlockSpec((1, tall, w), lambda bi: (bi, 0, 0))],
        out_specs=pl.BlockSpec((1, tall, w), lambda bi: (bi, 0, 0)),
        out_shape=jax.ShapeDtypeStruct((b, tall, w), I32),
        scratch_shapes=[pltpu.VMEM((tall, w), F32), pltpu.VMEM((tall, w), F32)],
        compiler_params=_cparams(("parallel",)),
        name="expert_choice_route",
    )(aff3)


def _expert_kernel(h_ref, pos_ref, aff_ref, wg_ref, wu_ref, wd_ref, o_ref, xe_scr, acc_scr, val_scr, *, ns):
    f = pl.program_id(2)

    @pl.when(f == 0)
    def _():
        pos = pos_ref[0, 0]
        hit = pos == _iota((ns, 1), 0)
        onehot = jnp.where(hit, 1.0, 0.0).astype(BF16)
        xe_scr[...] = jnp.dot(onehot, h_ref[0], preferred_element_type=F32).astype(BF16)
        val_scr[...] = jnp.sum(jnp.where(hit, aff_ref[0, 0], 0.0), axis=1, keepdims=True)
        acc_scr[...] = jnp.zeros_like(acc_scr)

    xe = xe_scr[...]
    g = jnp.dot(xe, wg_ref[0], preferred_element_type=F32)
    u = jnp.dot(xe, wu_ref[0], preferred_element_type=F32)
    hid = (_silu(g) * u).astype(BF16)
    acc_scr[...] += jnp.dot(hid, wd_ref[0], preferred_element_type=F32)

    @pl.when(f == pl.num_programs(2) - 1)
    def _():
        o_ref[0, 0] = (acc_scr[...] * val_scr[...]).astype(o_ref.dtype)


def expert_call(h3, pos_t, aff_t, w_gate, w_up, w_down, ns):
    b, tall, d = h3.shape
    e, _, ff = w_gate.shape
    tf = _pick(ff, 512, LANES)
    return pl.pallas_call(
        functools.partial(_expert_kernel, ns=ns),
        grid=(b, e, ff // tf),
        in_specs=[pl.BlockSpec((1, tall, d), lambda bi, ei, f: (bi, 0, 0), pipeline_mode=pl.Buffered(1)),
                  pl.BlockSpec((1, 1, 1, tall), lambda bi, ei, f: (bi, ei, 0, 0)),
                  pl.BlockSpec((1, 1, 1, tall), lambda bi, ei, f: (bi, ei, 0, 0)),
                  pl.BlockSpec((1, d, tf), lambda bi, ei, f: (ei, 0, f)),
                  pl.BlockSpec((1, d, tf), lambda bi, ei, f: (ei, 0, f)),
                  pl.BlockSpec((1, tf, d), lambda bi, ei, f: (ei, f, 0))],
        out_specs=pl.BlockSpec((1, 1, ns, d), lambda bi, ei, f: (bi, ei, 0, 0)),
        out_shape=jax.ShapeDtypeStruct((b, e, ns, d), BF16),
        scratch_shapes=[pltpu.VMEM((ns, d), BF16), pltpu.VMEM((ns, d), F32), pltpu.VMEM((ns, 1), F32)],
        compiler_params=_cparams(("parallel", "arbitrary", "arbitrary"), 56),
        name="expert_ffn",
    )(h3, pos_t, aff_t, w_gate, w_up, w_down)


def _combine_kernel(pos_ref, ye_ref, x_ref, gc_ref, gl_ref, o_ref, acc_scr, *, ns, tt, l_ctx):
    e = pl.program_id(2)

    @pl.when(e == 0)
    def _():
        acc_scr[...] = jnp.zeros_like(acc_scr)

    pos = pos_ref[0]
    col = jnp.sum(jnp.where(_iota(pos.shape, 1) == e, pos, 0), axis=1, keepdims=True)
    onehot = jnp.where(col == _iota((1, ns), 1), 1.0, 0.0).astype(BF16)
    acc_scr[...] += jnp.dot(onehot, ye_ref[0, 0], preferred_element_type=F32)

    @pl.when(e == pl.num_programs(2) - 1)
    def _():
        row = pl.program_id(1) * tt + _iota((tt, 1), 0)
        gate = jnp.where(row < l_ctx, gc_ref[0, 0, 0], gl_ref[0, 0, 0])
        o_ref[0] = x_ref[0] + gate * acc_scr[...]


def combine_call(pos3, ye, xall, mod5, l_ctx):
    b, tall, d = xall.shape
    _, e, ns, _ = ye.shape
    tt = _pick(tall, 1088, 16)
    gspec = lambda seg: pl.BlockSpec((1, 1, 1, 1, d), lambda bi, i, ei: (bi, seg, 5, 0, 0))
    return pl.pallas_call(
        functools.partial(_combine_kernel, ns=ns, tt=tt, l_ctx=l_ctx),
        grid=(b, tall // tt, e),
        in_specs=[pl.BlockSpec((1, tt, LANES), lambda bi, i, ei: (bi, i, 0)),
                  pl.BlockSpec((1, 1, ns, d), lambda bi, i, ei: (bi, ei, 0, 0)),
                  pl.BlockSpec((1, tt, d), lambda bi, i, ei: (bi, i, 0)),
                  gspec(0), gspec(1)],
        out_specs=pl.BlockSpec((1, tt, d), lambda bi, i, ei: (bi, i, 0)),
        out_shape=jax.ShapeDtypeStruct((b, tall, d), F32),
        scratch_shapes=[pltpu.VMEM((tt, d), F32)],
        compiler_params=_cparams(("parallel", "parallel", "arbitrary"), 56),
        name="expert_combine",
    )(pos3, ye, xall, mod5, mod5)


def _rope_tables(l_ctx, t_lat):
    t = jnp.arange(t_lat)
    n_freq = HEAD_DIM // 4
    inv = ROPE_THETA ** (-jnp.arange(n_freq, dtype=F32) / n_freq)
    ang = jnp.concatenate([(t // GRID_W).astype(F32)[:, None] * inv, (t % GRID_W).astype(F32)[:, None] * inv], axis=-1)
    cos, sin = jnp.cos(ang), jnp.sin(ang)
    cos_t = jnp.concatenate([jnp.ones((l_ctx, HEAD_DIM), F32), jnp.concatenate([cos, cos], axis=-1)], axis=0)
    sin_t = jnp.concatenate([jnp.zeros((l_ctx, HEAD_DIM), F32), jnp.concatenate([-sin, sin], axis=-1)], axis=0)
    return cos_t, sin_t


def kernel(x, c, ctx, c_ctx, w_mod, b_mod, norm1_g, norm2_g, w_in, pool_w, pool_scale, q_norm_g, k_norm_g,
           conv_w, a_log, dt_bias, gdn_norm_g, w_branch, w_out, w_router, w_e_gate, w_e_up, w_e_down, final_g):
    b, t_lat, d = x.shape
    l_ctx = ctx.shape[1]
    tall = l_ctx + t_lat
    depth = w_mod.shape[0]
    heads = d // 256
    hw = heads * HEAD_DIM
    kvw = ATT_KV_HEADS * HEAD_DIM
    pool_width = d // 2
    n_exp = w_router.shape[-1]
    m = b * tall
    rb = _pick(math.gcd(l_ctx, t_lat), 256, 16)
    nblk = tall // rb

    names = ("u", "aq", "ak", "av", "gq", "gk", "gv", "gz", "ga", "gb", "gate")
    widths = (pool_width, hw, kvw, kvw, hw, hw, hw, hw, 2 * heads, 2 * heads, N_BRANCH * d)
    src, acc = {}, 0
    for nme, wd in zip(names, widths):
        src[nme] = (acc, wd)
        acc += wd
    order = ("u", "aq", "gq", "gk", "gv", "gz", "gate", "ak", "av")
    offs, acc = {}, 0
    for nme in order:
        offs[nme] = acc
        acc += src[nme][1]
    n_main = acc

    cos_t, sin_t = _rope_tables(l_ctx, t_lat)
    cstack = jnp.zeros((8, d), F32).at[:b].set(c).at[b].set(c_ctx)
    xall = jnp.concatenate([ctx, x], axis=1)

    cap_c = CAPACITY_FACTOR * l_ctx // n_exp
    cap_l = CAPACITY_FACTOR * t_lat // n_exp
    ns = cap_c + cap_l
    segs = ((0, l_ctx, cap_c, 0), (l_ctx, t_lat, cap_l, cap_c))
    is_ctx_blk = (jnp.arange(nblk) < l_ctx // rb)[None, :, None]

    def table(mod6, kk):
        lat = mod6[:b, kk][:, None, :]
        con = jnp.broadcast_to(mod6[b, kk][None, None, :], (b, 1, d))
        return jnp.where(is_ctx_blk, con, lat).reshape(b * nblk, 1, d)

    for l in range(depth):
        wl = w_in[l]
        w_main = jnp.concatenate([wl[:, src[n][0]:src[n][0] + src[n][1]] for n in order], axis=1).astype(BF16)
        w_ab = jnp.concatenate([wl[:, src["ga"][0]:src["gb"][0] + src["gb"][1]],
                                jnp.zeros((d, LANES - 4 * heads), F32)], axis=1).astype(BF16)
        w_rt = jnp.pad(w_router[l], ((0, 0), (0, LANES - n_exp)))

        mod6 = mod_call(cstack, w_mod[l], b_mod[l]).reshape(8, 6, d)
        mod5 = jnp.stack([jnp.broadcast_to(mod6[b][None], (b, 6, d)), mod6[:b]], axis=1).reshape(b, 2, 6, 1, d)

        h = norm_call(xall.reshape(m, d), norm1_g[l], table(mod6, 0), table(mod6, 1), rb)
        p2 = matmul_call(h, w_main, BF16, name="in_proj")
        pab = matmul_call(h, w_ab, F32, name="in_proj_ab")
        p3 = p2.reshape(b, tall, n_main)

        pool = pool_call(p3, pool_w[l].astype(BF16), pool_scale[l], l_ctx, t_lat)
        att = attn_call(p3, cos_t, sin_t, q_norm_g[l], k_norm_g[l], offs, l_ctx, t_lat, heads)

        gates = gdn_gate_call(pab.reshape(b, tall, LANES), a_log[l], dt_bias[l])
        gsp = gates[:, :, :4 * heads].reshape(b, tall, 2, 2, heads)
        gcol = jnp.transpose(gsp, (3, 0, 1, 2, 4)).reshape(2, b, tall // CHUNK, CHUNK, 2 * heads)
        grow = jnp.swapaxes(gcol, 3, 4)
        qn, kn, vn = gdn_prep_call(p3, conv_w[l], offs, l_ctx, heads)
        o2 = gdn_scan_call(qn, kn, vn, gcol, grow, l_ctx, heads)
        gdn = gdn_out_call(o2, p3, gdn_norm_g[l], offs, heads)

        acc2 = merge_call((pool.reshape(m, -1), att.reshape(m, -1), gdn.reshape(m, -1)), p2,
                          w_branch[l].astype(BF16), offs["gate"])
        x2 = outproj_call(acc2, w_out[l].astype(BF16), xall.reshape(m, d), table(mod6, 2), rb)

        h2, aff = norm_call(x2, norm2_g[l], table(mod6, 3), table(mod6, 4), rb, w_router=w_rt, n_exp=n_exp)
        aff3 = aff.reshape(b, tall, LANES)
        pos3 = route_call(aff3, segs)
        pos_t = jnp.swapaxes(pos3[:, :, :n_exp], 1, 2).reshape(b, n_exp, 1, tall)
        aff_t = jnp.swapaxes(aff3[:, :, :n_exp], 1, 2).reshape(b, n_exp, 1, tall)
        ye = expert_call(h2.reshape(b, tall, d), pos_t, aff_t, w_e_gate[l].astype(BF16), w_e_up[l].astype(BF16),
                         w_e_down[l].astype(BF16), ns)
        xall = combine_call(pos3, ye, x2.reshape(b, tall, d), mod5, l_ctx)

    return final_norm_call(xall, final_g, l_ctx, t_lat)
```
